```python
import math
import jax, jax.numpy as jnp
from jax import lax
import numpy as np

D_MODEL = 2048
BATCH = 2
SEQ = 16384
DEPTH = 4

CHUNK = 64
N_META = 16
N_MIXERS = 2
N_CONV_LAYERS = (DEPTH + 1) // 2
N_ATTN_LAYERS = DEPTH // 2
CONV_WIDTH = 31
N_HEADS = 8
HEAD_DIM = 128
D_ATTN = N_HEADS * 2 * HEAD_DIM
Q_BLOCK = 128
SCALE = 1.0 / math.sqrt(HEAD_DIM)
N_BUCKETS = 32
MAX_DISTANCE = 128
FFN_CONV_WIDTH = 3
D_FF = ((8 * D_MODEL // 3 + 127) // 128) * 128
NORM_EPS = 1e-6
NEG = -1e30

kernel_name = "hybrid_conformer_diffattn_convffn_trunk"


def rms_norm(x, g):
    x32 = x.astype(jnp.float32)
    y = x32 * lax.rsqrt(jnp.mean(x32 * x32, axis=-1, keepdims=True) + NORM_EPS)
    return (y * g.astype(jnp.float32)).astype(x.dtype)


def layer_norm(x, g, b):
    x32 = x.astype(jnp.float32)
    mu = jnp.mean(x32, axis=-1, keepdims=True)
    xc = x32 - mu
    var = jnp.mean(xc * xc, axis=-1, keepdims=True)
    y = xc * lax.rsqrt(var + NORM_EPS)
    return (y * g.astype(jnp.float32) + b.astype(jnp.float32)).astype(x.dtype)


def causal_dwconv(x, w, b):
    k_width, c = w.shape
    y = lax.conv_general_dilated(
        x, w.astype(x.dtype)[:, None, :], window_strides=(1,), padding=[(k_width - 1, 0)],
        dimension_numbers=("NWC", "WIO", "NWC"), feature_group_count=c)
    return y + b.astype(x.dtype)


def chunk_ids(pos):
    return jnp.where(pos < N_META, 0, 1 + (pos - N_META) // CHUNK)


def t5_bucket(rel):
    half = N_BUCKETS // 2
    max_exact = half // 2
    ret = jnp.where(rel > 0, half, 0)
    n = jnp.abs(rel)
    nf = jnp.maximum(n, 1).astype(jnp.float32)
    large = max_exact + (jnp.log(nf / max_exact) / math.log(MAX_DISTANCE / max_exact)
                         * (half - max_exact)).astype(jnp.int32)
    large = jnp.minimum(large, half - 1)
    return ret + jnp.where(n < max_exact, n, large)


def conformer_conv(h, w_in, b_in, dw_w, dw_b, ln_g, ln_b, w_out, b_out):
    u = h @ w_in + b_in
    a, g = jnp.split(u, 2, axis=-1)
    u = a * jax.nn.sigmoid(g)
    u = causal_dwconv(u, dw_w, dw_b)
    u = jax.nn.silu(layer_norm(u, ln_g, ln_b))
    return u @ w_out + b_out


def diff_attention(h, w_qkv, lam_p, subln_g, w_o, rel_bias, layer_number):
    bsz, seq_len, _ = h.shape
    qkv = h @ w_qkv
    q = qkv[..., :D_ATTN].reshape(bsz, seq_len, N_HEADS, 2, HEAD_DIM)
    k = qkv[..., D_ATTN:2 * D_ATTN].reshape(bsz, seq_len, N_HEADS, 2, HEAD_DIM)
    v = qkv[..., 2 * D_ATTN:].reshape(bsz, seq_len, N_HEADS, 2 * HEAD_DIM)
    lp = ((seq_len + Q_BLOCK - 1) // Q_BLOCK) * Q_BLOCK
    pad = lp - seq_len
    q = jnp.pad(q, ((0, 0), (0, pad), (0, 0), (0, 0), (0, 0)))
    k = jnp.pad(k, ((0, 0), (0, pad), (0, 0), (0, 0), (0, 0)))
    v = jnp.pad(v, ((0, 0), (0, pad), (0, 0), (0, 0)))
    nb = lp // Q_BLOCK
    q_blocks = q.reshape(bsz, nb, Q_BLOCK, N_HEADS, 2, HEAD_DIM).transpose(1, 0, 2, 3, 4, 5)
    pos = jnp.arange(lp, dtype=jnp.int32)
    k_chunk = chunk_ids(pos)
    k_valid = pos < seq_len
    lam_init = 0.8 - 0.6 * math.exp(-0.3 * (layer_number - 1))
    lp32 = lam_p.astype(jnp.float32)
    lam = (jnp.exp(jnp.sum(lp32[0] * lp32[1])) - jnp.exp(jnp.sum(lp32[2] * lp32[3])) + lam_init)
    table = rel_bias.astype(jnp.float32)

    def block(args):
        qb, start = args
        q_pos = start + jnp.arange(Q_BLOCK, dtype=jnp.int32)
        s = jnp.einsum('bqhmd,bkhmd->bhmqk', qb, k, preferred_element_type=jnp.float32) * SCALE
        bias = table[t5_bucket(pos[None, :] - q_pos[:, None])].transpose(2, 0, 1)
        mask = (k_chunk[None, :] <= chunk_ids(q_pos)[:, None]) & k_valid[None, :]
        s = jnp.where(mask, s + bias[None, :, None], NEG)
        p = jax.nn.softmax(s, axis=-1)
        a = p[:, :, 0] - lam * p[:, :, 1]
        return jnp.einsum('bhqk,bkhe->bqhe', a.astype(v.dtype), v)

    starts = jnp.arange(nb, dtype=jnp.int32) * Q_BLOCK
    o = lax.map(block, (q_blocks, starts))
    o = o.transpose(1, 0, 2, 3, 4).reshape(bsz, lp, N_HEADS, 2 * HEAD_DIM)[:, :seq_len]
    o = (rms_norm(o, subln_g) * (1.0 - lam_init)).astype(h.dtype)
    return o.reshape(bsz, seq_len, D_ATTN) @ w_o


def conv_ffn(h, w_in, dw_w, dw_b, w_out):
    u = h @ w_in
    u = causal_dwconv(u, dw_w, dw_b)
    val, gate = jnp.split(u, 2, axis=-1)
    return (jax.nn.silu(gate) * val) @ w_out


def setup_inputs(seed: int = 0) -> dict:
    key = jax.random.key(seed)
    ks = jax.random.split(key, 24)
    f32 = jnp.float32
    nrm = lambda i, shape, s: jax.random.normal(ks[i], shape, f32) * s
    gain = lambda i, shape: 1.0 + 0.05 * jax.random.normal(ks[i], shape, f32)
    NC, NA = N_CONV_LAYERS, N_ATTN_LAYERS
    return {
        "x": nrm(0, (BATCH, SEQ, D_MODEL), 1.0),
        "meta_tokens": nrm(1, (N_META, D_MODEL), 1.0),
        "rel_bias": nrm(2, (N_BUCKETS, N_HEADS), 0.5),
        "mix_pre_g": gain(3, (DEPTH, D_MODEL)),
        "mix_post_g": gain(4, (DEPTH, D_MODEL)),
        "ffn_pre_g": gain(5, (DEPTH, D_MODEL)),
        "ffn_post_g": gain(6, (DEPTH, D_MODEL)),
        "conv_w_in": nrm(7, (NC, D_MODEL, 2 * D_MODEL), D_MODEL ** -0.5),
        "conv_b_in": nrm(8, (NC, 2 * D_MODEL), 0.02),
        "conv_dw_w": nrm(9, (NC, CONV_WIDTH, D_MODEL), CONV_WIDTH ** -0.5),
        "conv_dw_b": nrm(10, (NC, D_MODEL), 0.02),
        "conv_ln_g": gain(11, (NC, D_MODEL)),
        "conv_ln_b": nrm(12, (NC, D_MODEL), 0.02),
        "conv_w_out": nrm(13, (NC, D_MODEL, D_MODEL), D_MODEL ** -0.5),
        "conv_b_out": nrm(14, (NC, D_MODEL), 0.02),
        "attn_w_qkv": nrm(15, (NA, D_MODEL, 3 * D_ATTN), D_MODEL ** -0.5),
        "attn_lambda": nrm(16, (NA, 4, HEAD_DIM), 0.1),
        "attn_subln_g": gain(17, (NA, 2 * HEAD_DIM)),
        "attn_w_o": nrm(18, (NA, D_ATTN, D_MODEL), D_ATTN ** -0.5),
        "ffn_w_in": nrm(19, (DEPTH, D_MODEL, 2 * D_FF), D_MODEL ** -0.5),
        "ffn_dw_w": nrm(20, (DEPTH, FFN_CONV_WIDTH, 2 * D_FF), FFN_CONV_WIDTH ** -0.5),
        "ffn_dw_b": nrm(21, (DEPTH, 2 * D_FF), 0.02),
        "ffn_w_out": nrm(22, (DEPTH, D_FF, D_MODEL), D_FF ** -0.5),
    }


def reference(x, meta_tokens, rel_bias, mix_pre_g, mix_post_g, ffn_pre_g, ffn_post_g,
              conv_w_in, conv_b_in, conv_dw_w, conv_dw_b, conv_ln_g, conv_ln_b, conv_w_out, conv_b_out,
              attn_w_qkv, attn_lambda, attn_subln_g, attn_w_o,
              ffn_w_in, ffn_dw_w, ffn_dw_b, ffn_w_out):
    bsz = x.shape[0]
    meta = jnp.broadcast_to(meta_tokens.astype(x.dtype)[None], (bsz, N_META, x.shape[-1]))
    h = jnp.concatenate([meta, x], axis=1)
    for i in range(DEPTH):
        j = i // N_MIXERS
        u = rms_norm(h, mix_pre_g[i])
        if i % N_MIXERS == 0:
            u = conformer_conv(u, conv_w_in[j], conv_b_in[j], conv_dw_w[j], conv_dw_b[j],
                               conv_ln_g[j], conv_ln_b[j], conv_w_out[j], conv_b_out[j])
        else:
            u = diff_attention(u, attn_w_qkv[j], attn_lambda[j], attn_subln_g[j], attn_w_o[j],
                               rel_bias, i + 1)
        h = h + rms_norm(u, mix_post_g[i])
        u = conv_ffn(rms_norm(h, ffn_pre_g[i]), ffn_w_in[i], ffn_dw_w[i], ffn_dw_b[i], ffn_w_out[i])
        h = h + rms_norm(u, ffn_post_g[i])
    return h[:, N_META:]
```

```python
import functools
import math

import numpy as np
import jax
import jax.numpy as jnp
from jax import lax
from jax.experimental import pallas as pl
from jax.experimental.pallas import tpu as pltpu

D_MODEL = 2048
CHUNK = 64
N_META = 16
CONV_WIDTH = 31
N_HEADS = 8
HEAD_DIM = 128
D_ATTN = N_HEADS * 2 * HEAD_DIM
SCALE = 1.0 / math.sqrt(HEAD_DIM)
N_BUCKETS = 32
MAX_DISTANCE = 128
FFN_CONV_WIDTH = 3
D_FF = 5504
NORM_EPS = 1e-6
NEG = -1e30

V7X_LANES = 128
V7X_SUBLANES = 8
V7X_BF16_ROWS = 16
V7X_VMEM_LIMIT_BYTES = 56 * 1024 * 1024

ROW_TILE = 512
PAD_ROWS = ROW_TILE
META_ROW0 = PAD_ROWS - N_META
SUB = 128
N_SUB = ROW_TILE // SUB
CONV_HALO = 32
FFN_HALO = V7X_BF16_ROWS
CONV_ROW_BLOCK = 64
CONV_COL_BLOCK = 256
FF_CHUNK = 512
D_FF_PAD = -(-D_FF // FF_CHUNK) * FF_CHUNK
GLU_COL_TILE = 1024
QKV_COL_TILE = D_ATTN

F32 = jnp.float32
BF16 = jnp.bfloat16


def _params(*semantics):
    return pltpu.CompilerParams(dimension_semantics=semantics, vmem_limit_bytes=V7X_VMEM_LIMIT_BYTES)


def _rms(x, g):
    ms = jnp.mean(x * x, axis=-1, keepdims=True)
    return x * lax.rsqrt(ms + NORM_EPS) * g


def _sigmoid(x):
    return 1.0 / (1.0 + jnp.exp(-x))


def _row_ids(tile_index, rows):
    return tile_index * rows + lax.broadcasted_iota(jnp.int32, (rows, 1), 0)


def _qkv_kernel(x_ref, g_ref, w_ref, o_ref, xn_ref):
    j = pl.program_id(2)

    @pl.when(j == 0)
    def _():
        xn_ref[...] = _rms(x_ref[0], g_ref[...]).astype(BF16)

    y = jnp.dot(xn_ref[...], w_ref[...], preferred_element_type=F32)
    y = y * jnp.where(j == 0, SCALE, 1.0)
    o_ref[0] = y.astype(BF16)


def _qkv_proj(h, g, w):
    bsz, rows, d = h.shape
    n = w.shape[1]
    return pl.pallas_call(
        _qkv_kernel,
        out_shape=jax.ShapeDtypeStruct((bsz, rows, n), BF16),
        grid=(bsz, rows // ROW_TILE, n // QKV_COL_TILE),
        in_specs=[
            pl.BlockSpec((1, ROW_TILE, d), lambda b, i, j: (b, i, 0)),
            pl.BlockSpec((1, d), lambda b, i, j: (0, 0)),
            pl.BlockSpec((d, QKV_COL_TILE), lambda b, i, j: (0, j)),
        ],
        out_specs=pl.BlockSpec((1, ROW_TILE, QKV_COL_TILE), lambda b, i, j: (b, i, j)),
        scratch_shapes=[pltpu.VMEM((ROW_TILE, d), BF16)],
        compiler_params=_params("parallel", "parallel", "arbitrary"),
        name="qkv_proj",
    )(h, g, w)


def _glu_kernel(x_ref, g_ref, wa_ref, wg_ref, ba_ref, bg_ref, o_ref, xn_ref):
    i = pl.program_id(1)
    j = pl.program_id(2)

    @pl.when(j == 0)
    def _():
        xn_ref[...] = _rms(x_ref[0], g_ref[...]).astype(BF16)

    xn = xn_ref[...]
    a = jnp.dot(xn, wa_ref[...], preferred_element_type=F32) + ba_ref[...]
    gate = jnp.dot(xn, wg_ref[...], preferred_element_type=F32) + bg_ref[...]
    u = a * _sigmoid(gate)
    o_ref[0] = jnp.where(_row_ids(i, ROW_TILE) >= META_ROW0, u, 0.0)


def _glu_proj(h, g, w_in, b_in):
    bsz, rows, d = h.shape
    n_col = d // GLU_COL_TILE
    return pl.pallas_call(
        _glu_kernel,
        out_shape=jax.ShapeDtypeStruct((bsz, rows, d), F32),
        grid=(bsz, rows // ROW_TILE, n_col),
        in_specs=[
            pl.BlockSpec((1, ROW_TILE, d), lambda b, i, j: (b, i, 0)),
            pl.BlockSpec((1, d), lambda b, i, j: (0, 0)),
            pl.BlockSpec((d, GLU_COL_TILE), lambda b, i, j: (0, j)),
            pl.BlockSpec((d, GLU_COL_TILE), lambda b, i, j: (0, j + n_col)),
            pl.BlockSpec((1, GLU_COL_TILE), lambda b, i, j: (0, j)),
            pl.BlockSpec((1, GLU_COL_TILE), lambda b, i, j: (0, j + n_col)),
        ],
        out_specs=pl.BlockSpec((1, ROW_TILE, GLU_COL_TILE), lambda b, i, j: (b, i, j)),
        scratch_shapes=[pltpu.VMEM((ROW_TILE, d), BF16)],
        compiler_params=_params("parallel", "parallel", "arbitrary"),
        name="glu_proj",
    )(h, g, w_in, w_in, b_in, b_in)


def _conv_kernel(u_ref, halo_ref, h_ref, dww_ref, dwb_ref, lng_ref, lnb_ref, wo_ref, bo_ref, gp_ref,
                 o_ref, ext_ref, y_ref):
    i = pl.program_id(1)
    d = u_ref.shape[2]
    ext_ref[0:CONV_HALO, :] = jnp.where(i > 0, halo_ref[0], 0.0)
    ext_ref[CONV_HALO:, :] = u_ref[0]

    first = CONV_HALO - (CONV_WIDTH - 1)

    def col_body(c, carry):
        c0 = pl.multiple_of(c * CONV_COL_BLOCK, CONV_COL_BLOCK)
        cols = pl.ds(c0, CONV_COL_BLOCK)
        for rb in range(ROW_TILE // CONV_ROW_BLOCK):
            acc = None
            for rho in range(V7X_SUBLANES):
                extra = V7X_SUBLANES if rho else 0
                group = None
                for k in range(CONV_WIDTH):
                    s = first + k
                    if s % V7X_SUBLANES != rho:
                        continue
                    win = ext_ref[pl.ds(rb * CONV_ROW_BLOCK + s - rho, CONV_ROW_BLOCK + extra), cols]
                    term = win * dww_ref[pl.ds(k, 1), cols]
                    group = term if group is None else group + term
                part = group[rho:rho + CONV_ROW_BLOCK, :]
                acc = part if acc is None else acc + part
            y_ref[pl.ds(rb * CONV_ROW_BLOCK, CONV_ROW_BLOCK), cols] = acc + dwb_ref[:, cols]
        return carry

    lax.fori_loop(0, d // CONV_COL_BLOCK, col_body, 0)

    y = y_ref[...]
    mu = jnp.mean(y, axis=-1, keepdims=True)
    yc = y - mu
    var = jnp.mean(yc * yc, axis=-1, keepdims=True)
    z = yc * lax.rsqrt(var + NORM_EPS) * lng_ref[...] + lnb_ref[...]
    z = z * _sigmoid(z)
    o2 = jnp.dot(z.astype(BF16), wo_ref[...], preferred_element_type=F32) + bo_ref[...]
    o_ref[0] = h_ref[0] + _rms(o2, gp_ref[...])


def _conv_mix(u, h, dw_w, dw_b, ln_g, ln_b, w_out, b_out, g_post):
    bsz, rows, d = h.shape
    halo_per_tile = ROW_TILE // CONV_HALO
    row = lambda b, i: (0, 0)
    return pl.pallas_call(
        _conv_kernel,
        out_shape=jax.ShapeDtypeStruct((bsz, rows, d), F32),
        grid=(bsz, rows // ROW_TILE),
        in_specs=[
            pl.BlockSpec((1, ROW_TILE, d), lambda b, i: (b, i, 0)),
            pl.BlockSpec((1, CONV_HALO, d), lambda b, i: (b, jnp.maximum(i * halo_per_tile - 1, 0), 0)),
            pl.BlockSpec((1, ROW_TILE, d), lambda b, i: (b, i, 0)),
            pl.BlockSpec((CONV_WIDTH, d), row),
            pl.BlockSpec((1, d), row),
            pl.BlockSpec((1, d), row),
            pl.BlockSpec((1, d), row),
            pl.BlockSpec((d, d), row),
            pl.BlockSpec((1, d), row),
            pl.BlockSpec((1, d), row),
        ],
        out_specs=pl.BlockSpec((1, ROW_TILE, d), lambda b, i: (b, i, 0)),
        scratch_shapes=[pltpu.VMEM((ROW_TILE + CONV_HALO, d), F32), pltpu.VMEM((ROW_TILE, d), F32)],
        compiler_params=_params("parallel", "parallel"),
        name="conv_mix",
    )(u, u, h, dw_w, dw_b, ln_g, ln_b, w_out, b_out, g_post)


def _oproj_kernel(a_ref, h_ref, w_ref, g_ref, o_ref):
    y = jnp.dot(a_ref[0], w_ref[...], preferred_element_type=F32)
    o_ref[0] = h_ref[0] + _rms(y, g_ref[...])


def _out_proj(a, h, w, g):
    bsz, rows, d = h.shape
    return pl.pallas_call(
        _oproj_kernel,
        out_shape=jax.ShapeDtypeStruct((bsz, rows, d), F32),
        grid=(bsz, rows // ROW_TILE),
        in_specs=[
            pl.BlockSpec((1, ROW_TILE, a.shape[2]), lambda b, i: (b, i, 0)),
            pl.BlockSpec((1, ROW_TILE, d), lambda b, i: (b, i, 0)),
            pl.BlockSpec((a.shape[2], d), lambda b, i: (0, 0)),
            pl.BlockSpec((1, d), lambda b, i: (0, 0)),
        ],
        out_specs=pl.BlockSpec((1, ROW_TILE, d), lambda b, i: (b, i, 0)),
        compiler_params=_params("parallel", "parallel"),
        name="attn_out_proj",
    )(a, h, w, g)


def _ffn_kernel(h_ref, halo_ref, gpre_ref, wv_ref, wg_ref, dwv_ref, bv_ref, dwg_ref, bg_ref, wo_ref, gpost_ref,
                o_ref, xn_ref, *, n_chunks):
    i = pl.program_id(1)
    c = pl.program_id(2)

    @pl.when(c == 0)
    def _():
        xn = _rms(h_ref[0], gpre_ref[...])
        xn = jnp.where(_row_ids(i, ROW_TILE) >= META_ROW0, xn, 0.0)
        xn_ref[FFN_HALO:, :] = xn.astype(BF16)
        hn = _rms(halo_ref[0], gpre_ref[...])
        xn_ref[0:FFN_HALO, :] = jnp.where(i > 0, hn, 0.0).astype(BF16)

    xe = xn_ref[...]

    def branch(w_ref, dw_ref, b_ref):
        u = jnp.dot(xe, w_ref[...], preferred_element_type=F32)
        y = (dw_ref[2:3, :] * u + dw_ref[1:2, :] * pltpu.roll(u, 1, 0) + dw_ref[0:1, :] * pltpu.roll(u, 2, 0)
             + b_ref[...])
        return y[FFN_HALO:, :]

    yv = branch(wv_ref, dwv_ref, bv_ref)
    yg = branch(wg_ref, dwg_ref, bg_ref)
    act = (yg * _sigmoid(yg)) * yv
    contrib = jnp.dot(act.astype(BF16), wo_ref[...], preferred_element_type=F32)

    @pl.when(c == 0)
    def _():
        o_ref[0] = contrib

    @pl.when(c > 0)
    def _():
        o_ref[0] += contrib

    @pl.when(c == n_chunks - 1)
    def _():
        o_ref[0] = h_ref[0] + _rms(o_ref[0], gpost_ref[...])


def _conv_ffn(h, g_pre, w_val, w_gate, dw_val, b_val, dw_gate, b_gate, w_out, g_post, drop_tiles=0):
    bsz, rows, d = h.shape
    n_chunks = w_val.shape[1] // FF_CHUNK
    halo_per_tile = ROW_TILE // FFN_HALO
    vec = lambda b, i, c: (0, 0)
    ck = lambda b, i, c: (0, c)
    return pl.pallas_call(
        functools.partial(_ffn_kernel, n_chunks=n_chunks),
        out_shape=jax.ShapeDtypeStruct((bsz, rows - drop_tiles * ROW_TILE, d), F32),
        grid=(bsz, rows // ROW_TILE, n_chunks),
        in_specs=[
            pl.BlockSpec((1, ROW_TILE, d), lambda b, i, c: (b, i, 0)),
            pl.BlockSpec((1, FFN_HALO, d), lambda b, i, c: (b, jnp.maximum(i * halo_per_tile - 1, 0), 0)),
            pl.BlockSpec((1, d), vec),
            pl.BlockSpec((d, FF_CHUNK), ck),
            pl.BlockSpec((d, FF_CHUNK), ck),
            pl.BlockSpec((FFN_CONV_WIDTH, FF_CHUNK), ck),
            pl.BlockSpec((1, FF_CHUNK), ck),
            pl.BlockSpec((FFN_CONV_WIDTH, FF_CHUNK), ck),
            pl.BlockSpec((1, FF_CHUNK), ck),
            pl.BlockSpec((FF_CHUNK, d), lambda b, i, c: (c, 0)),
            pl.BlockSpec((1, d), vec),
        ],
        out_specs=pl.BlockSpec((1, ROW_TILE, d), lambda b, i, c: (b, jnp.maximum(i - drop_tiles, 0), 0)),
        scratch_shapes=[pltpu.VMEM((FFN_HALO + ROW_TILE, d), BF16)],
        compiler_params=_params("parallel", "arbitrary", "arbitrary"),
        name="conv_ffn",
    )(h, h, g_pre, w_val, w_gate, dw_val, b_val, dw_gate, b_gate, w_out, g_post)


def _bucket_thresholds():
    half = N_BUCKETS // 2
    max_exact = half // 2
    out = []
    for j in range(1, half - max_exact):
        n = max_exact
        while n * n < max_exact * max_exact * 2 ** j:
            n += 1
        out.append(n)
    assert MAX_DISTANCE == max_exact * 2 ** ((half - max_exact) // 2)
    return max_exact, half, out


def _bias_kernel(table_ref, o_ref):
    max_exact, half, thresholds = _bucket_thresholds()
    qi = lax.broadcasted_iota(jnp.int32, (SUB, SUB), 0)
    kj = lax.broadcasted_iota(jnp.int32, (SUB, SUB), 1)

    def bucket(rel):
        n = jnp.abs(rel)
        large = jnp.full_like(n, max_exact)
        for t in thresholds:
            large = large + (n >= t).astype(jnp.int32)
        return jnp.where(rel > 0, half, 0) + jnp.where(n < max_exact, n, large)

    b_diag = bucket(kj - qi)
    b_left = bucket(kj - qi - SUB)
    chunk_shift = CHUNK.bit_length() - 1
    visible = lax.shift_right_logical(kj, chunk_shift) <= lax.shift_right_logical(qi, chunk_shift)
    zeros = jnp.zeros((SUB, SUB), F32)
    masked = jnp.full((SUB, SUB), NEG, F32)
    far_bucket = half - 1
    for h in range(N_HEADS):
        d0 = zeros
        d1 = zeros
        for b in range(N_BUCKETS):
            d0 = jnp.where(b_diag == b, table_ref[b, h], d0)
            d1 = jnp.where(b_left == b, table_ref[b, h], d1)
        d0 = d0 - table_ref[far_bucket, h]
        d1 = d1 - table_ref[far_bucket, h]
        for t in range(4):
            o_ref[0, h, t] = zeros
        o_ref[1, h, 0] = zeros
        o_ref[1, h, 1] = zeros
        o_ref[1, h, 2] = zeros
        o_ref[1, h, 3] = d1
        o_ref[2, h, 0] = jnp.where(visible, d0, NEG)
        o_ref[2, h, 1] = d1
        o_ref[2, h, 2] = masked
        o_ref[2, h, 3] = masked


def _bias_blocks(rel_bias):
    return pl.pallas_call(
        _bias_kernel,
        out_shape=jax.ShapeDtypeStruct((3, N_HEADS, 4, SUB, SUB), F32),
        in_specs=[pl.BlockSpec(memory_space=pltpu.SMEM)],
        out_specs=pl.BlockSpec(memory_space=pltpu.VMEM),
        name="rel_bias_blocks",
    )(rel_bias)


def _tile_lanes(x, reps):
    return jnp.concatenate([x] * reps, axis=1)


def _attn_kernel(qi_ref, ki_ref, q_ref, k_ref, v_ref, nb_ref, cm_ref, lam_ref, sg_ref, o_ref,
                 acc_ref, m_ref, l_ref, *, lam_init):
    step = pl.program_id(1)
    qi = qi_ref[step]
    ki = ki_ref[step]
    tq = q_ref.shape[1]
    tk = k_ref.shape[1]
    dv = 2 * HEAD_DIM

    @pl.when(ki == 0)
    def _():
        m_ref[...] = jnp.full(m_ref.shape, -3.0e38, F32)
        l_ref[...] = jnp.zeros(l_ref.shape, F32)
        acc_ref[...] = jnp.zeros(acc_ref.shape, F32)

    def update(h, extra):
        v_h = v_ref[0, :, h * dv:(h + 1) * dv]
        for mp in range(2):
            j = 2 * h + mp
            c0 = h * dv + mp * HEAD_DIM
            s = lax.dot_general(q_ref[0, :, c0:c0 + HEAD_DIM], k_ref[0, :, c0:c0 + HEAD_DIM],
                                (((1,), (1,)), ((), ())), preferred_element_type=F32)
            if extra is not None:
                s = s + extra
            m_prev = m_ref[j]
            m_new = jnp.maximum(m_prev, jnp.max(s, axis=1, keepdims=True))
            alpha = jnp.exp(m_prev - m_new)
            p = jnp.exp(s - _tile_lanes(m_new, tk // V7X_LANES))
            l_ref[j] = alpha * l_ref[j] + jnp.sum(p, axis=1, keepdims=True)
            m_ref[j] = m_new
            pv = jnp.dot(p.astype(BF16), v_h, preferred_element_type=F32)
            acc_ref[j] = acc_ref[j] * _tile_lanes(alpha, dv // V7X_LANES) + pv

    near = jnp.logical_or(ki == 0, ki >= qi - 1)

    @pl.when(jnp.logical_not(near))
    def _():
        for h in range(N_HEADS):
            update(h, None)

    @pl.when(near)
    def _():
        cm = cm_ref[0]
        zero = jnp.zeros((SUB, SUB), F32)
        for h in range(N_HEADS):
            x0, x1, x2, x3 = nb_ref[0, h, 0], nb_ref[0, h, 1], nb_ref[0, h, 2], nb_ref[0, h, 3]
            rows = []
            for a in range(N_SUB):
                blocks = []
                for b in range(N_SUB):
                    if b == a:
                        blocks.append(x0)
                    elif b == a - 1:
                        blocks.append(x1)
                    elif b < a:
                        blocks.append(zero)
                    elif a == 0 and b == N_SUB - 1:
                        blocks.append(x3)
                    else:
                        blocks.append(x2)
                rows.append(jnp.concatenate(blocks, axis=1))
            update(h, jnp.concatenate(rows, axis=0) + cm)

    @pl.when(ki == qi)
    def _():
        lp = lam_ref[...]
        lam = (jnp.exp(jnp.sum(lp[0:1] * lp[1:2], axis=1, keepdims=True))
               - jnp.exp(jnp.sum(lp[2:3] * lp[3:4], axis=1, keepdims=True)) + lam_init)
        for h in range(N_HEADS):
            inv1 = _tile_lanes(1.0 / l_ref[2 * h], dv // V7X_LANES)
            inv2 = _tile_lanes(1.0 / l_ref[2 * h + 1], dv // V7X_LANES)
            o = acc_ref[2 * h] * inv1 - lam * (acc_ref[2 * h + 1] * inv2)
            y = _rms(o, sg_ref[...]) * (1.0 - lam_init)
            o_ref[0, :, h * dv:(h + 1) * dv] = y.astype(o_ref.dtype)


def _attention(qkv, bias_blocks, lam_p, subln_g, layer_number):
    bsz, rows, _ = qkv.shape
    nt = rows // ROW_TILE
    pairs = [(q, k) for q in range(nt) for k in range(q + 1)]
    q_idx = jnp.asarray(np.array([p[0] for p in pairs], np.int32))
    k_idx = jnp.asarray(np.array([p[1] for p in pairs], np.int32))
    col_mask = np.zeros((2, 1, ROW_TILE), np.float32)
    col_mask[1, 0, :META_ROW0] = NEG
    lam_init = 0.8 - 0.6 * math.exp(-0.3 * (layer_number - 1))

    def kind(b, s, qi, ki):
        return jnp.where(ki[s] == qi[s], 2, jnp.where(ki[s] == qi[s] - 1, 1, 0))

    grid_spec = pltpu.PrefetchScalarGridSpec(
        num_scalar_prefetch=2,
        grid=(bsz, len(pairs)),
        in_specs=[
            pl.BlockSpec((1, ROW_TILE, D_ATTN), lambda b, s, qi, ki: (b, qi[s], 0)),
            pl.BlockSpec((1, ROW_TILE, D_ATTN), lambda b, s, qi, ki: (b, ki[s], 1)),
            pl.BlockSpec((1, ROW_TILE, D_ATTN), lambda b, s, qi, ki: (b, ki[s], 2)),
            pl.BlockSpec((1, N_HEADS, 4, SUB, SUB), lambda b, s, qi, ki: (kind(b, s, qi, ki), 0, 0, 0, 0)),
            pl.BlockSpec((1, 1, ROW_TILE), lambda b, s, qi, ki: (jnp.where(ki[s] == 0, 1, 0), 0, 0)),
            pl.BlockSpec((4, HEAD_DIM), lambda b, s, qi, ki: (0, 0)),
            pl.BlockSpec((1, 2 * HEAD_DIM), lambda b, s, qi, ki: (0, 0)),
        ],
        out_specs=pl.BlockSpec((1, ROW_TILE, D_ATTN), lambda b, s, qi, ki: (b, qi[s], 0)),
        scratch_shapes=[
            pltpu.VMEM((2 * N_HEADS, ROW_TILE, 2 * HEAD_DIM), F32),
            pltpu.VMEM((2 * N_HEADS, ROW_TILE, V7X_LANES), F32),
            pltpu.VMEM((2 * N_HEADS, ROW_TILE, V7X_LANES), F32),
        ],
    )
    return pl.pallas_call(
        functools.partial(_attn_kernel, lam_init=lam_init),
        out_shape=jax.ShapeDtypeStruct((bsz, rows, D_ATTN), BF16),
        grid_spec=grid_spec,
        compiler_params=_params("parallel", "arbitrary"),
        name="diff_attention",
    )(q_idx, k_idx, qkv, qkv, qkv, bias_blocks, jnp.asarray(col_mask), lam_p, subln_g)


def _pad_cols(a, n):
    return jnp.pad(a, ((0, 0), (0, n - a.shape[1])))


def kernel(x, meta_tokens, rel_bias, mix_pre_g, mix_post_g, ffn_pre_g, ffn_post_g, conv_w_in, conv_b_in, conv_dw_w, conv_dw_b, conv_ln_g, conv_ln_b, conv_w_out, conv_b_out, attn_w_qkv, attn_lambda, attn_subln_g, attn_w_o, ffn_w_in, ffn_dw_w, ffn_dw_b, ffn_w_out):
    bsz, seq, d = x.shape
    assert d == D_MODEL and seq % ROW_TILE == 0
    depth = mix_pre_g.shape[0]
    row = lambda a: a.reshape(1, -1).astype(F32)

    meta = jnp.broadcast_to(meta_tokens.astype(x.dtype)[None], (bsz, N_META, d))
    h = jnp.concatenate([jnp.zeros((bsz, META_ROW0, d), x.dtype), meta, x], axis=1)

    bias_blocks = _bias_blocks(rel_bias.astype(F32))

    for i in range(depth):
        j = i // 2
        if i % 2 == 0:
            u = _glu_proj(h, row(mix_pre_g[i]), conv_w_in[j].astype(BF16), row(conv_b_in[j]))
            h = _conv_mix(u, h, conv_dw_w[j].astype(F32), row(conv_dw_b[j]), row(conv_ln_g[j]), row(conv_ln_b[j]),
                          conv_w_out[j].astype(BF16), row(conv_b_out[j]), row(mix_post_g[i]))
        else:
            qkv = _qkv_proj(h, row(mix_pre_g[i]), attn_w_qkv[j].astype(BF16))
            a = _attention(qkv, bias_blocks, attn_lambda[j].astype(F32), row(attn_subln_g[j]), i + 1)
            h = _out_proj(a, h, attn_w_o[j].astype(BF16), row(mix_post_g[i]))
        w_in = ffn_w_in[i]
        dw = ffn_dw_w[i].astype(F32)
        db = ffn_dw_b[i].astype(F32).reshape(1, -1)
        h = _conv_ffn(
            h, row(ffn_pre_g[i]),
            _pad_cols(w_in[:, :D_FF], D_FF_PAD).astype(BF16), _pad_cols(w_in[:, D_FF:], D_FF_PAD).astype(BF16),
            _pad_cols(dw[:, :D_FF], D_FF_PAD), _pad_cols(db[:, :D_FF], D_FF_PAD),
            _pad_cols(dw[:, D_FF:], D_FF_PAD), _pad_cols(db[:, D_FF:], D_FF_PAD),
            jnp.pad(ffn_w_out[i], ((0, D_FF_PAD - D_FF), (0, 0))).astype(BF16), row(ffn_post_g[i]),
            drop_tiles=PAD_ROWS // ROW_TILE if i == depth - 1 else 0)
    return h
```

```python
import functools
import math

import numpy as np
import jax
import jax.numpy as jnp
from jax import lax
from jax.experimental import pallas as pl
from jax.experimental.pallas import tpu as pltpu

D_MODEL = 2048
CHUNK = 64
N_META = 16
CONV_WIDTH = 31
N_HEADS = 8
HEAD_DIM = 128
D_ATTN = N_HEADS * 2 * HEAD_DIM
SCALE = 1.0 / math.sqrt(HEAD_DIM)
LOG2E = math.log2(math.e)
N_BUCKETS = 32
MAX_DISTANCE = 128
FFN_CONV_WIDTH = 3
D_FF = 5504
NORM_EPS = 1e-6
NEG = -1e30

V7X_LANES = 128
V7X_SUBLANES = 8
V7X_BF16_ROWS = 16
V7X_VMEM_LIMIT_BYTES = 56 * 1024 * 1024

ROW_TILE = 512
PAD_ROWS = ROW_TILE
META_ROW0 = PAD_ROWS - N_META
SUB = 128
N_SUB = ROW_TILE // SUB
CONV_HALO = 32
FFN_HALO = V7X_BF16_ROWS
CONV_ROW_BLOCK = 64
CONV_COL_BLOCK = 256
FF_CHUNK = 512
D_FF_PAD = -(-D_FF // FF_CHUNK) * FF_CHUNK
GLU_COL_TILE = 1024
QKV_COL_TILE = D_ATTN

F32 = jnp.float32
BF16 = jnp.bfloat16


def _params(*semantics):
    return pltpu.CompilerParams(dimension_semantics=semantics, vmem_limit_bytes=V7X_VMEM_LIMIT_BYTES)


def _rms(x, g):
    ms = jnp.mean(x * x, axis=-1, keepdims=True)
    return x * lax.rsqrt(ms + NORM_EPS) * g


def _sigmoid(x):
    return 1.0 / (1.0 + jnp.exp(-x))


def _row_ids(tile_index, rows):
    return tile_index * rows + lax.broadcasted_iota(jnp.int32, (rows, 1), 0)


def _qkv_kernel(x_ref, g_ref, w_ref, o_ref, xn_ref):
    j = pl.program_id(2)

    @pl.when(j == 0)
    def _():
        xn_ref[...] = _rms(x_ref[0], g_ref[...]).astype(BF16)

    y = jnp.dot(xn_ref[...], w_ref[...], preferred_element_type=F32)
    y = y * jnp.where(j == 0, SCALE * LOG2E, 1.0)
    o_ref[0] = y.astype(BF16)


def _qkv_proj(h, g, w):
    bsz, rows, d = h.shape
    n = w.shape[1]
    return pl.pallas_call(
        _qkv_kernel,
        out_shape=jax.ShapeDtypeStruct((bsz, rows, n), BF16),
        grid=(bsz, rows // ROW_TILE, n // QKV_COL_TILE),
        in_specs=[
            pl.BlockSpec((1, ROW_TILE, d), lambda b, i, j: (b, i, 0)),
            pl.BlockSpec((1, d), lambda b, i, j: (0, 0)),
            pl.BlockSpec((d, QKV_COL_TILE), lambda b, i, j: (0, j)),
        ],
        out_specs=pl.BlockSpec((1, ROW_TILE, QKV_COL_TILE), lambda b, i, j: (b, i, j)),
        scratch_shapes=[pltpu.VMEM((ROW_TILE, d), BF16)],
        compiler_params=_params("parallel", "parallel", "arbitrary"),
        name="qkv_proj",
    )(h, g, w)


def _glu_kernel(x_ref, g_ref, wa_ref, wg_ref, ba_ref, bg_ref, o_ref, xn_ref):
    i = pl.program_id(1)
    j = pl.program_id(2)

    @pl.when(j == 0)
    def _():
        xn_ref[...] = _rms(x_ref[0], g_ref[...]).astype(BF16)

    xn = xn_ref[...]
    a = jnp.dot(xn, wa_ref[...], preferred_element_type=F32) + ba_ref[...]
    gate = jnp.dot(xn, wg_ref[...], preferred_element_type=F32) + bg_ref[...]
    u = a * _sigmoid(gate)
    o_ref[0] = jnp.where(_row_ids(i, ROW_TILE) >= META_ROW0, u, 0.0)


def _glu_proj(h, g, w_in, b_in):
    bsz, rows, d = h.shape
    n_col = d // GLU_COL_TILE
    return pl.pallas_call(
        _glu_kernel,
        out_shape=jax.ShapeDtypeStruct((bsz, rows, d), F32),
        grid=(bsz, rows // ROW_TILE, n_col),
        in_specs=[
            pl.BlockSpec((1, ROW_TILE, d), lambda b, i, j: (b, i, 0)),
            pl.BlockSpec((1, d), lambda b, i, j: (0, 0)),
            pl.BlockSpec((d, GLU_COL_TILE), lambda b, i, j: (0, j)),
            pl.BlockSpec((d, GLU_COL_TILE), lambda b, i, j: (0, j + n_col)),
            pl.BlockSpec((1, GLU_COL_TILE), lambda b, i, j: (0, j)),
            pl.BlockSpec((1, GLU_COL_TILE), lambda b, i, j: (0, j + n_col)),
        ],
        out_specs=pl.BlockSpec((1, ROW_TILE, GLU_COL_TILE), lambda b, i, j: (b, i, j)),
        scratch_shapes=[pltpu.VMEM((ROW_TILE, d), BF16)],
        compiler_params=_params("parallel", "parallel", "arbitrary"),
        name="glu_proj",
    )(h, g, w_in, w_in, b_in, b_in)


def _conv_kernel(u_ref, halo_ref, h_ref, dww_ref, dwb_ref, lng_ref, lnb_ref, wo_ref, bo_ref, gp_ref,
                 o_ref, ext_ref, y_ref):
    i = pl.program_id(1)
    d = u_ref.shape[2]
    ext_ref[0:CONV_HALO, :] = jnp.where(i > 0, halo_ref[0], 0.0)
    ext_ref[CONV_HALO:, :] = u_ref[0]

    first = CONV_HALO - (CONV_WIDTH - 1)

    def col_body(c, carry):
        c0 = pl.multiple_of(c * CONV_COL_BLOCK, CONV_COL_BLOCK)
        cols = pl.ds(c0, CONV_COL_BLOCK)
        for rb in range(ROW_TILE // CONV_ROW_BLOCK):
            acc = None
            for rho in range(V7X_SUBLANES):
                extra = V7X_SUBLANES if rho else 0
                group = None
                for k in range(CONV_WIDTH):
                    s = first + k
                    if s % V7X_SUBLANES != rho:
                        continue
                    win = ext_ref[pl.ds(rb * CONV_ROW_BLOCK + s - rho, CONV_ROW_BLOCK + extra), cols]
                    term = win * dww_ref[pl.ds(k, 1), cols]
                    group = term if group is None else group + term
                part = group[rho:rho + CONV_ROW_BLOCK, :]
                acc = part if acc is None else acc + part
            y_ref[pl.ds(rb * CONV_ROW_BLOCK, CONV_ROW_BLOCK), cols] = acc + dwb_ref[:, cols]
        return carry

    lax.fori_loop(0, d // CONV_COL_BLOCK, col_body, 0)

    y = y_ref[...]
    mu = jnp.mean(y, axis=-1, keepdims=True)
    yc = y - mu
    var = jnp.mean(yc * yc, axis=-1, keepdims=True)
    z = yc * lax.rsqrt(var + NORM_EPS) * lng_ref[...] + lnb_ref[...]
    z = z * _sigmoid(z)
    o2 = jnp.dot(z.astype(BF16), wo_ref[...], preferred_element_type=F32) + bo_ref[...]
    o_ref[0] = h_ref[0] + _rms(o2, gp_ref[...])


def _conv_mix(u, h, dw_w, dw_b, ln_g, ln_b, w_out, b_out, g_post):
    bsz, rows, d = h.shape
    halo_per_tile = ROW_TILE // CONV_HALO
    row = lambda b, i: (0, 0)
    return pl.pallas_call(
        _conv_kernel,
        out_shape=jax.ShapeDtypeStruct((bsz, rows, d), F32),
        grid=(bsz, rows // ROW_TILE),
        in_specs=[
            pl.BlockSpec((1, ROW_TILE, d), lambda b, i: (b, i, 0)),
            pl.BlockSpec((1, CONV_HALO, d), lambda b, i: (b, jnp.maximum(i * halo_per_tile - 1, 0), 0)),
            pl.BlockSpec((1, ROW_TILE, d), lambda b, i: (b, i, 0)),
            pl.BlockSpec((CONV_WIDTH, d), row),
            pl.BlockSpec((1, d), row),
            pl.BlockSpec((1, d), row),
            pl.BlockSpec((1, d), row),
            pl.BlockSpec((d, d), row),
            pl.BlockSpec((1, d), row),
            pl.BlockSpec((1, d), row),
        ],
        out_specs=pl.BlockSpec((1, ROW_TILE, d), lambda b, i: (b, i, 0)),
        scratch_shapes=[pltpu.VMEM((ROW_TILE + CONV_HALO, d), F32), pltpu.VMEM((ROW_TILE, d), F32)],
        compiler_params=_params("parallel", "parallel"),
        name="conv_mix",
    )(u, u, h, dw_w, dw_b, ln_g, ln_b, w_out, b_out, g_post)


def _oproj_kernel(a_ref, h_ref, w_ref, g_ref, o_ref):
    y = jnp.dot(a_ref[0], w_ref[...], preferred_element_type=F32)
    o_ref[0] = h_ref[0] + _rms(y, g_ref[...])


def _out_proj(a, h, w, g):
    bsz, rows, d = h.shape
    return pl.pallas_call(
        _oproj_kernel,
        out_shape=jax.ShapeDtypeStruct((bsz, rows, d), F32),
        grid=(bsz, rows // ROW_TILE),
        in_specs=[
            pl.BlockSpec((1, ROW_TILE, a.shape[2]), lambda b, i: (b, i, 0)),
            pl.BlockSpec((1, ROW_TILE, d), lambda b, i: (b, i, 0)),
            pl.BlockSpec((a.shape[2], d), lambda b, i: (0, 0)),
            pl.BlockSpec((1, d), lambda b, i: (0, 0)),
        ],
        out_specs=pl.BlockSpec((1, ROW_TILE, d), lambda b, i: (b, i, 0)),
        compiler_params=_params("parallel", "parallel"),
        name="attn_out_proj",
    )(a, h, w, g)


def _ffn_kernel(h_ref, halo_ref, gpre_ref, wv_ref, wg_ref, dwv_ref, bv_ref, dwg_ref, bg_ref, wo_ref, gpost_ref,
                o_ref, xn_ref, act_ref, *, n_chunks):
    i = pl.program_id(1)
    c = pl.program_id(2)
    slot = lax.rem(c, 2)

    def up():
        xe = xn_ref[...]

        def branch(w_ref, dw_ref, b_ref):
            u = jnp.dot(xe, w_ref[...], preferred_element_type=F32)
            y = (dw_ref[2:3, :] * u + dw_ref[1:2, :] * pltpu.roll(u, 1, 0) + dw_ref[0:1, :] * pltpu.roll(u, 2, 0)
                 + b_ref[...])
            return y[FFN_HALO:, :]

        yv = branch(wv_ref, dwv_ref, bv_ref)
        yg = branch(wg_ref, dwg_ref, bg_ref)
        act_ref[slot] = ((yg * _sigmoid(yg)) * yv).astype(BF16)

    def down():
        return jnp.dot(act_ref[1 - slot], wo_ref[...], preferred_element_type=F32)

    @pl.when(c == 0)
    def _():
        xn = _rms(h_ref[0], gpre_ref[...])
        xn = jnp.where(_row_ids(i, ROW_TILE) >= META_ROW0, xn, 0.0)
        xn_ref[FFN_HALO:, :] = xn.astype(BF16)
        hn = _rms(halo_ref[0], gpre_ref[...])
        xn_ref[0:FFN_HALO, :] = jnp.where(i > 0, hn, 0.0).astype(BF16)
        o_ref[0] = jnp.zeros(o_ref.shape[1:], F32)
        up()

    @pl.when(jnp.logical_and(c > 0, c < n_chunks))
    def _():
        o_ref[0] += down()
        up()

    @pl.when(c == n_chunks)
    def _():
        o_ref[0] = h_ref[0] + _rms(o_ref[0] + down(), gpost_ref[...])


def _conv_ffn(h, g_pre, w_val, w_gate, dw_val, b_val, dw_gate, b_gate, w_out, g_post, drop_tiles=0):
    bsz, rows, d = h.shape
    n_chunks = w_val.shape[1] // FF_CHUNK
    halo_per_tile = ROW_TILE // FFN_HALO
    vec = lambda b, i, c: (0, 0)
    ck = lambda b, i, c: (0, jnp.minimum(c, n_chunks - 1))
    return pl.pallas_call(
        functools.partial(_ffn_kernel, n_chunks=n_chunks),
        out_shape=jax.ShapeDtypeStruct((bsz, rows - drop_tiles * ROW_TILE, d), F32),
        grid=(bsz, rows // ROW_TILE, n_chunks + 1),
        in_specs=[
            pl.BlockSpec((1, ROW_TILE, d), lambda b, i, c: (b, i, 0)),
            pl.BlockSpec((1, FFN_HALO, d), lambda b, i, c: (b, jnp.maximum(i * halo_per_tile - 1, 0), 0)),
            pl.BlockSpec((1, d), vec),
            pl.BlockSpec((d, FF_CHUNK), ck),
            pl.BlockSpec((d, FF_CHUNK), ck),
            pl.BlockSpec((FFN_CONV_WIDTH, FF_CHUNK), ck),
            pl.BlockSpec((1, FF_CHUNK), ck),
            pl.BlockSpec((FFN_CONV_WIDTH, FF_CHUNK), ck),
            pl.BlockSpec((1, FF_CHUNK), ck),
            pl.BlockSpec((FF_CHUNK, d), lambda b, i, c: (jnp.maximum(c - 1, 0), 0)),
            pl.BlockSpec((1, d), vec),
        ],
        out_specs=pl.BlockSpec((1, ROW_TILE, d), lambda b, i, c: (b, jnp.maximum(i - drop_tiles, 0), 0)),
        scratch_shapes=[pltpu.VMEM((FFN_HALO + ROW_TILE, d), BF16), pltpu.VMEM((2, ROW_TILE, FF_CHUNK), BF16)],
        compiler_params=_params("parallel", "arbitrary", "arbitrary"),
        name="conv_ffn",
    )(h, h, g_pre, w_val, w_gate, dw_val, b_val, dw_gate, b_gate, w_out, g_post)


def _bucket_thresholds():
    half = N_BUCKETS // 2
    max_exact = half // 2
    out = []
    for j in range(1, half - max_exact):
        n = max_exact
        while n * n < max_exact * max_exact * 2 ** j:
            n += 1
        out.append(n)
    assert MAX_DISTANCE == max_exact * 2 ** ((half - max_exact) // 2)
    return max_exact, half, out


def _bias_kernel(table_ref, o_ref):
    max_exact, half, thresholds = _bucket_thresholds()
    qi = lax.broadcasted_iota(jnp.int32, (SUB, SUB), 0)
    kj = lax.broadcasted_iota(jnp.int32, (SUB, SUB), 1)

    def bucket(rel):
        n = jnp.abs(rel)
        large = jnp.full_like(n, max_exact)
        for t in thresholds:
            large = large + (n >= t).astype(jnp.int32)
        return jnp.where(rel > 0, half, 0) + jnp.where(n < max_exact, n, large)

    b_diag = bucket(kj - qi)
    b_left = bucket(kj - qi - SUB)
    chunk_shift = CHUNK.bit_length() - 1
    visible = lax.shift_right_logical(kj, chunk_shift) <= lax.shift_right_logical(qi, chunk_shift)
    zeros = jnp.zeros((SUB, SUB), F32)
    masked = jnp.full((SUB, SUB), NEG, F32)
    far_bucket = half - 1
    for h in range(N_HEADS):
        d0 = zeros
        d1 = zeros
        for b in range(N_BUCKETS):
            d0 = jnp.where(b_diag == b, table_ref[b, h], d0)
            d1 = jnp.where(b_left == b, table_ref[b, h], d1)
        d0 = (d0 - table_ref[far_bucket, h]) * LOG2E
        d1 = (d1 - table_ref[far_bucket, h]) * LOG2E
        for t in range(4):
            o_ref[0, h, t] = zeros
        o_ref[1, h, 0] = zeros
        o_ref[1, h, 1] = zeros
        o_ref[1, h, 2] = zeros
        o_ref[1, h, 3] = d1
        o_ref[2, h, 0] = jnp.where(visible, d0, NEG)
        o_ref[2, h, 1] = d1
        o_ref[2, h, 2] = masked
        o_ref[2, h, 3] = masked


def _bias_blocks(rel_bias):
    return pl.pallas_call(
        _bias_kernel,
        out_shape=jax.ShapeDtypeStruct((3, N_HEADS, 4, SUB, SUB), F32),
        in_specs=[pl.BlockSpec(memory_space=pltpu.SMEM)],
        out_specs=pl.BlockSpec(memory_space=pltpu.VMEM),
        name="rel_bias_blocks",
    )(rel_bias)


def _tile_lanes(x, reps):
    return jnp.concatenate([x] * reps, axis=1)


def _attn_kernel(qi_ref, ki_ref, q_ref, k_ref, v_ref, nb_ref, cm_ref, lam_ref, sg_ref, o_ref,
                 acc_ref, m_ref, l_ref, *, lam_init):
    step = pl.program_id(1)
    qi = qi_ref[step]
    ki = ki_ref[step]
    tq = q_ref.shape[1]
    tk = k_ref.shape[1]
    dv = 2 * HEAD_DIM

    @pl.when(ki == 0)
    def _():
        m_ref[...] = jnp.full(m_ref.shape, -3.0e38, F32)
        l_ref[...] = jnp.zeros(l_ref.shape, F32)
        acc_ref[...] = jnp.zeros(acc_ref.shape, F32)

    def update(h, extra):
        v_h = v_ref[0, :, h * dv:(h + 1) * dv]
        for mp in range(2):
            j = 2 * h + mp
            c0 = h * dv + mp * HEAD_DIM
            s = lax.dot_general(q_ref[0, :, c0:c0 + HEAD_DIM], k_ref[0, :, c0:c0 + HEAD_DIM],
                                (((1,), (1,)), ((), ())), preferred_element_type=F32)
            if extra is not None:
                s = s + extra
            m_prev = m_ref[j]
            m_new = jnp.maximum(m_prev, jnp.max(s, axis=1, keepdims=True))
            alpha = jnp.exp2(m_prev - m_new)
            p = jnp.exp2(s - _tile_lanes(m_new, tk // V7X_LANES))
            l_ref[j] = alpha * l_ref[j] + jnp.sum(p, axis=1, keepdims=True)
            m_ref[j] = m_new
            pv = jnp.dot(p.astype(BF16), v_h, preferred_element_type=F32)
            acc_ref[j] = acc_ref[j] * _tile_lanes(alpha, dv // V7X_LANES) + pv

    near = jnp.logical_or(ki == 0, ki >= qi - 1)

    @pl.when(jnp.logical_not(near))
    def _():
        for h in range(N_HEADS):
            update(h, None)

    @pl.when(near)
    def _():
        cm = cm_ref[0]
        zero = jnp.zeros((SUB, SUB), F32)
        for h in range(N_HEADS):
            x0, x1, x2, x3 = nb_ref[0, h, 0], nb_ref[0, h, 1], nb_ref[0, h, 2], nb_ref[0, h, 3]
            rows = []
            for a in range(N_SUB):
                blocks = []
                for b in range(N_SUB):
                    if b == a:
                        blocks.append(x0)
                    elif b == a - 1:
                        blocks.append(x1)
                    elif b < a:
                        blocks.append(zero)
                    elif a == 0 and b == N_SUB - 1:
                        blocks.append(x3)
                    else:
                        blocks.append(x2)
                rows.append(jnp.concatenate(blocks, axis=1))
            update(h, jnp.concatenate(rows, axis=0) + cm)

    @pl.when(ki == qi)
    def _():
        lp = lam_ref[...]
        lam = (jnp.exp(jnp.sum(lp[0:1] * lp[1:2], axis=1, keepdims=True))
               - jnp.exp(jnp.sum(lp[2:3] * lp[3:4], axis=1, keepdims=True)) + lam_init)
        for h in range(N_HEADS):
            inv1 = _tile_lanes(1.0 / l_ref[2 * h], dv // V7X_LANES)
            inv2 = _tile_lanes(1.0 / l_ref[2 * h + 1], dv // V7X_LANES)
            o = acc_ref[2 * h] * inv1 - lam * (acc_ref[2 * h + 1] * inv2)
            y = _rms(o, sg_ref[...]) * (1.0 - lam_init)
            o_ref[0, :, h * dv:(h + 1) * dv] = y.astype(o_ref.dtype)


def _attention(qkv, bias_blocks, lam_p, subln_g, layer_number):
    bsz, rows, _ = qkv.shape
    nt = rows // ROW_TILE
    pairs = [(q, k) for q in range(nt) for k in range(q + 1)]
    q_idx = jnp.asarray(np.array([p[0] for p in pairs], np.int32))
    k_idx = jnp.asarray(np.array([p[1] for p in pairs], np.int32))
    col_mask = np.zeros((2, 1, ROW_TILE), np.float32)
    col_mask[1, 0, :META_ROW0] = NEG
    lam_init = 0.8 - 0.6 * math.exp(-0.3 * (layer_number - 1))

    def kind(s, qi, ki):
        return jnp.where(ki[s] == qi[s], 2, jnp.where(ki[s] == qi[s] - 1, 1, 0))

    grid_spec = pltpu.PrefetchScalarGridSpec(
        num_scalar_prefetch=2,
        grid=(bsz, len(pairs)),
        in_specs=[
            pl.BlockSpec((1, ROW_TILE, D_ATTN), lambda b, s, qi, ki: (b, qi[s], 0)),
            pl.BlockSpec((1, ROW_TILE, D_ATTN), lambda b, s, qi, ki: (b, ki[s], 1)),
            pl.BlockSpec((1, ROW_TILE, D_ATTN), lambda b, s, qi, ki: (b, ki[s], 2)),
            pl.BlockSpec((1, N_HEADS, 4, SUB, SUB), lambda b, s, qi, ki: (kind(s, qi, ki), 0, 0, 0, 0)),
            pl.BlockSpec((1, 1, ROW_TILE), lambda b, s, qi, ki: (jnp.where(ki[s] == 0, 1, 0), 0, 0)),
            pl.BlockSpec((4, HEAD_DIM), lambda b, s, qi, ki: (0, 0)),
            pl.BlockSpec((1, 2 * HEAD_DIM), lambda b, s, qi, ki: (0, 0)),
        ],
        out_specs=pl.BlockSpec((1, ROW_TILE, D_ATTN), lambda b, s, qi, ki: (b, qi[s], 0)),
        scratch_shapes=[
            pltpu.VMEM((2 * N_HEADS, ROW_TILE, 2 * HEAD_DIM), F32),
            pltpu.VMEM((2 * N_HEADS, ROW_TILE, V7X_LANES), F32),
            pltpu.VMEM((2 * N_HEADS, ROW_TILE, V7X_LANES), F32),
        ],
    )
    return pl.pallas_call(
        functools.partial(_attn_kernel, lam_init=lam_init),
        out_shape=jax.ShapeDtypeStruct((bsz, rows, D_ATTN), BF16),
        grid_spec=grid_spec,
        compiler_params=_params("parallel", "arbitrary"),
        name="diff_attention",
    )(q_idx, k_idx, qkv, qkv, qkv, bias_blocks, jnp.asarray(col_mask), lam_p, subln_g)


def _pad_cols(a, n):
    return jnp.pad(a, ((0, 0), (0, n - a.shape[1])))


def kernel(x, meta_tokens, rel_bias, mix_pre_g, mix_post_g, ffn_pre_g, ffn_post_g, conv_w_in, conv_b_in, conv_dw_w, conv_dw_b, conv_ln_g, conv_ln_b, conv_w_out, conv_b_out, attn_w_qkv, attn_lambda, attn_subln_g, attn_w_o, ffn_w_in, ffn_dw_w, ffn_dw_b, ffn_w_out):
    bsz, seq, d = x.shape
    assert d == D_MODEL and seq % ROW_TILE == 0
    depth = mix_pre_g.shape[0]
    row = lambda a: a.reshape(1, -1).astype(F32)

    meta = jnp.broadcast_to(meta_tokens.astype(x.dtype)[None], (bsz, N_META, d))
    h = jnp.concatenate([jnp.zeros((bsz, META_ROW0, d), x.dtype), meta, x], axis=1)

    bias_blocks = _bias_blocks(rel_bias.astype(F32))

    for i in range(depth):
        j = i // 2
        if i % 2 == 0:
            u = _glu_proj(h, row(mix_pre_g[i]), conv_w_in[j].astype(BF16), row(conv_b_in[j]))
            h = _conv_mix(u, h, conv_dw_w[j].astype(F32), row(conv_dw_b[j]), row(conv_ln_g[j]), row(conv_ln_b[j]),
                          conv_w_out[j].astype(BF16), row(conv_b_out[j]), row(mix_post_g[i]))
        else:
            qkv = _qkv_proj(h, row(mix_pre_g[i]), attn_w_qkv[j].astype(BF16))
            a = _attention(qkv, bias_blocks, attn_lambda[j].astype(F32), row(attn_subln_g[j]), i + 1)
            h = _out_proj(a, h, attn_w_o[j].astype(BF16), row(mix_post_g[i]))
        w_in = ffn_w_in[i]
        dw = ffn_dw_w[i].astype(F32)
        db = ffn_dw_b[i].astype(F32).reshape(1, -1)
        h = _conv_ffn(
            h, row(ffn_pre_g[i]),
            _pad_cols(w_in[:, :D_FF], D_FF_PAD).astype(BF16), _pad_cols(w_in[:, D_FF:], D_FF_PAD).astype(BF16),
            _pad_cols(dw[:, :D_FF], D_FF_PAD), _pad_cols(db[:, :D_FF], D_FF_PAD),
            _pad_cols(dw[:, D_FF:], D_FF_PAD), _pad_cols(db[:, D_FF:], D_FF_PAD),
            jnp.pad(ffn_w_out[i], ((0, D_FF_PAD - D_FF), (0, 0))).astype(BF16), row(ffn_post_g[i]),
            drop_tiles=PAD_ROWS // ROW_TILE if i == depth - 1 else 0)
    return h
```
